```python
import math
import jax, jax.numpy as jnp
from jax import lax
import numpy as np

D_MODEL = 1024
BATCH = 16
SEQ = 4096
DEPTH = 1
DEC_BATCH = 4
DEC_SEQ = 4096
PAST_LEN = 128

MEM_LEN = 256
MLA_HEADS = 8
Q_LORA = 384
KV_LORA = 256
QK_NOPE = 64
QK_ROPE = 32
V_HEAD = 64
ROPE_THETA = 10000.0
Q_BLOCK = 128
RWKV_HEADS = 8
RWKV_HEAD = 64
RWKV_DIM = RWKV_HEADS * RWKV_HEAD
DECAY_LORA = 64
AAA_LORA = 64
GATE_LORA = 128
X_HEADS = 4
X_HEAD = 128
D_FF = 2816
N_BRANCH = 2
LN_EPS = 1e-5
RMS_EPS = 1e-6
GN_EPS = 64e-5
ALPHA = (2 * DEPTH) ** 0.25
BETA = (8 * DEPTH) ** -0.25

SEG_Q = Q_LORA
SEG_KV = KV_LORA + QK_ROPE
SEG_RWKV = 3 * RWKV_DIM + 2 * DECAY_LORA + AAA_LORA + GATE_LORA
SEG_GATE = N_BRANCH * D_MODEL
D_IN = SEG_Q + SEG_KV + SEG_RWKV + SEG_GATE
OFF_KV = SEG_Q
OFF_RWKV = OFF_KV + SEG_KV
OFF_GATE = OFF_RWKV + SEG_RWKV
R_WD = 3 * RWKV_DIM
R_AD = R_WD + 2 * DECAY_LORA
R_GD = R_AD + AAA_LORA

kernel_name = 'hybrid_mla_rwkv7_macaron_deepnorm_encoder'


def _layernorm(x, g, b):
    xf = x.astype(jnp.float32)
    mu = jnp.mean(xf, -1, keepdims=True)
    var = jnp.mean(jnp.square(xf - mu), -1, keepdims=True)
    return ((xf - mu) * lax.rsqrt(var + LN_EPS) * g + b).astype(x.dtype)


def _rmsnorm(x, g):
    xf = x.astype(jnp.float32)
    return (xf * lax.rsqrt(jnp.mean(jnp.square(xf), -1, keepdims=True) + RMS_EPS) * g).astype(x.dtype)


def _swiglu(x, w_gu, w_down):
    gate, up = jnp.split(x @ w_gu, 2, axis=-1)
    return (jax.nn.silu(gate) * up) @ w_down


def _rope_tables(seq):
    inv = 1.0 / (ROPE_THETA ** (jnp.arange(0, QK_ROPE, 2, dtype=jnp.float32) / QK_ROPE))
    ang = jnp.arange(seq, dtype=jnp.float32)[:, None] * inv[None, :]
    return jnp.cos(ang), jnp.sin(ang)


def _rope(x, cos, sin):
    x1, x2 = jnp.split(x.astype(jnp.float32), 2, axis=-1)
    return jnp.concatenate([x1 * cos - x2 * sin, x1 * sin + x2 * cos], -1).astype(x.dtype)


def _mla(h_q, h_kv, q_norm_g, w_uq, kv_norm_g, w_ukv):
    b, s, _ = h_q.shape
    cos, sin = _rope_tables(s)
    q = (_rmsnorm(h_q, q_norm_g) @ w_uq).reshape(b, s, MLA_HEADS, QK_NOPE + QK_ROPE)
    q_nope = q[..., :QK_NOPE]
    q_rope = _rope(q[..., QK_NOPE:], cos[:, None, :], sin[:, None, :])
    c_kv = _rmsnorm(h_kv[..., :KV_LORA], kv_norm_g)
    k_rope = _rope(h_kv[..., KV_LORA:], cos, sin)
    kv = (c_kv @ w_ukv).reshape(b, s, MLA_HEADS, QK_NOPE + V_HEAD)
    k_nope, v = kv[..., :QK_NOPE], kv[..., QK_NOPE:]
    scale = (QK_NOPE + QK_ROPE) ** -0.5
    nblk = s // Q_BLOCK
    qn_b = q_nope.reshape(b, nblk, Q_BLOCK, MLA_HEADS, QK_NOPE).transpose(1, 0, 2, 3, 4)
    qr_b = q_rope.reshape(b, nblk, Q_BLOCK, MLA_HEADS, QK_ROPE).transpose(1, 0, 2, 3, 4)

    def block(args):
        qn, qr = args
        sc = (jnp.einsum('bqhd,bkhd->bhqk', qn, k_nope, preferred_element_type=jnp.float32)
              + jnp.einsum('bqhd,bkd->bhqk', qr, k_rope, preferred_element_type=jnp.float32))
        pr = jax.nn.softmax(sc * scale, axis=-1)
        return jnp.einsum('bhqk,bkhd->bqhd', pr.astype(v.dtype), v)

    o = lax.map(block, (qn_b, qr_b))
    return o.transpose(1, 0, 2, 3, 4).reshape(b, s, MLA_HEADS * V_HEAD)


def _centred_shift(h, mu_prev, mu_next):
    prev = jnp.pad(h[:, :-1], ((0, 0), (1, 0), (0, 0)))
    nxt = jnp.pad(h[:, 1:], ((0, 0), (0, 1), (0, 0)))
    return h + mu_prev * (prev - h) + mu_next * (nxt - h)


def _rwkv_scan(r, w, k, v, a, bb, reverse):
    _, bsz, nh, n = r.shape

    def step(S, inp):
        rt, wt, kt, vt, at, bt = inp
        Sa = jnp.einsum('bhij,bhj->bhi', S, at)
        S = S * wt[:, :, None, :] + Sa[..., None] * bt[:, :, None, :] + vt[..., None] * kt[:, :, None, :]
        return S, jnp.einsum('bhij,bhj->bhi', S, rt)

    S0 = jnp.zeros((bsz, nh, n, n), jnp.float32)
    _, y = lax.scan(step, S0, (r, w, k, v, a, bb), reverse=reverse)
    return y


def _rwkv7(h, mu_prev, mu_next, w0, w_up, a0, a_up, g_up, k_k, k_a, r_k, lnx_g, lnx_b):
    b, s, _ = h.shape
    f32 = jnp.float32
    h = _centred_shift(h, mu_prev, mu_next).astype(f32)
    r = h[..., :RWKV_DIM]
    k = h[..., RWKV_DIM:2 * RWKV_DIM]
    v = h[..., 2 * RWKV_DIM:3 * RWKV_DIM]
    wd = h[..., R_WD:R_AD].reshape(b, s, 2, DECAY_LORA)
    ad = h[..., R_AD:R_GD]
    gd = h[..., R_GD:SEG_RWKV]
    w_log = -jax.nn.softplus(-(w0 + jnp.einsum('bsdl,dlc->bsdc', jnp.tanh(wd), w_up))) - 0.5
    decay = jnp.exp(-jnp.exp(w_log.astype(f32))).reshape(b, s, 2, RWKV_HEADS, RWKV_HEAD)
    a = jax.nn.sigmoid(a0 + ad @ a_up)
    g = jax.nn.sigmoid(gd) @ g_up
    kk = (k * k_k).reshape(b, s, RWKV_HEADS, RWKV_HEAD)
    kk = kk / jnp.maximum(jnp.sqrt(jnp.sum(kk * kk, -1, keepdims=True)), 1e-12)
    k = k * (1.0 + (a - 1.0) * k_a)
    hs = lambda t: t.reshape(b, s, RWKV_HEADS, RWKV_HEAD)
    r, k, v, a = hs(r), hs(k), hs(v), hs(a)
    tm = lambda t: t.transpose(1, 0, 2, 3)
    rt, kt, vt = tm(r), tm(k), tm(v)
    a_vec, b_vec = tm(-kk), tm(kk * a)
    y = (_rwkv_scan(rt, tm(decay[:, :, 0]), kt, vt, a_vec, b_vec, False)
         + _rwkv_scan(rt, tm(decay[:, :, 1]), kt, vt, a_vec, b_vec, True))
    y = tm(y)
    mu = jnp.mean(y, -1, keepdims=True)
    var = jnp.mean(jnp.square(y - mu), -1, keepdims=True)
    yn = ((y - mu) * lax.rsqrt(var + GN_EPS)).reshape(b, s, RWKV_DIM) * lnx_g + lnx_b
    bonus = (jnp.sum(r * k * r_k, -1, keepdims=True) * v).reshape(b, s, RWKV_DIM)
    return (yn + bonus) * g


def _cross(x, mem, mem_g, mem_b, w_cq, w_ckv, w_co):
    b, s, _ = x.shape
    m = _layernorm(mem, mem_g, mem_b)
    q = (x @ w_cq).reshape(b, s, X_HEADS, X_HEAD)
    kv = (m @ w_ckv).reshape(b, mem.shape[1], 2, X_HEADS, X_HEAD)
    sc = jnp.einsum('bqhd,bkhd->bhqk', q, kv[:, :, 0], preferred_element_type=jnp.float32) * (X_HEAD ** -0.5)
    pr = jax.nn.softmax(sc, axis=-1)
    o = jnp.einsum('bhqk,bkhd->bqhd', pr.astype(x.dtype), kv[:, :, 1]).reshape(b, s, X_HEADS * X_HEAD)
    return o @ w_co


def _layer(x, mem, p):
    b, s, _ = x.shape
    x = _layernorm(ALPHA * x + 0.5 * _swiglu(x, p['ffn1_wgu'], p['ffn1_wd']), p['ln1_g'], p['ln1_b'])
    h = x @ p['w_in']
    br_a = _mla(h[..., :OFF_KV], h[..., OFF_KV:OFF_RWKV], p['q_norm_g'], p['w_uq'],
                p['kv_norm_g'], p['w_ukv']) @ p['p_mla']
    br_b = _rwkv7(h[..., OFF_RWKV:OFF_GATE], p['mu_prev'], p['mu_next'], p['w0'], p['w_up'],
                  p['a0'], p['a_up'], p['g_up'], p['k_k'], p['k_a'], p['r_k'],
                  p['lnx_g'], p['lnx_b']).astype(x.dtype) @ p['p_rwkv']
    gates = jax.nn.sigmoid(h[..., OFF_GATE:] + p['b_gate']).reshape(b, s, N_BRANCH, D_MODEL)
    mix = (gates[:, :, 0] * br_a + gates[:, :, 1] * br_b) @ p['w_o']
    x = _layernorm(ALPHA * x + mix, p['ln2_g'], p['ln2_b'])
    x = _layernorm(ALPHA * x + _cross(x, mem, p['mem_g'], p['mem_b'], p['w_cq'], p['w_ckv'], p['w_co']),
                   p['ln3_g'], p['ln3_b'])
    x = _layernorm(ALPHA * x + 0.5 * _swiglu(x, p['ffn2_wgu'], p['ffn2_wd']), p['ln4_g'], p['ln4_b'])
    return x


def setup_inputs(seed: int = 0) -> dict:
    key = jax.random.key(seed)
    ks = iter(jax.random.split(key, 64))

    def nrm(shape, scale):
        return scale * jax.random.normal(next(ks), shape, jnp.float32)

    L, D = DEPTH, D_MODEL
    mla_out = MLA_HEADS * V_HEAD
    x_out = X_HEADS * X_HEAD
    w0_base = jnp.linspace(-6.0, -1.0, RWKV_DIM, dtype=jnp.float32) + 0.5
    return {
        'x_prompt': nrm((BATCH, SEQ, D), 1.0),
        'x_sample': nrm((DEC_BATCH, DEC_SEQ, D), 1.0),
        'mem_prompt': nrm((BATCH, MEM_LEN, D), 1.0),
        'mem_sample': nrm((DEC_BATCH, MEM_LEN, D), 1.0),
        'ln1_g': 1.0 + nrm((L, D), 0.02), 'ln1_b': nrm((L, D), 0.02),
        'ffn1_wgu': nrm((L, D, 2 * D_FF), D ** -0.5),
        'ffn1_wd': nrm((L, D_FF, D), BETA * D_FF ** -0.5),
        'w_in': nrm((L, D, D_IN), D ** -0.5),
        'b_gate': nrm((L, SEG_GATE), 0.02),
        'q_norm_g': 1.0 + nrm((L, Q_LORA), 0.02),
        'w_uq': nrm((L, Q_LORA, MLA_HEADS * (QK_NOPE + QK_ROPE)), Q_LORA ** -0.5),
        'kv_norm_g': 1.0 + nrm((L, KV_LORA), 0.02),
        'w_ukv': nrm((L, KV_LORA, MLA_HEADS * (QK_NOPE + V_HEAD)), KV_LORA ** -0.5),
        'p_mla': nrm((L, mla_out, D), mla_out ** -0.5),
        'mu_prev': jax.random.uniform(next(ks), (L, SEG_RWKV), jnp.float32, 0.1, 0.6),
        'mu_next': jax.random.uniform(next(ks), (L, SEG_RWKV), jnp.float32, 0.1, 0.6),
        'w0': w0_base + nrm((L, 2, RWKV_DIM), 0.3),
        'w_up': nrm((L, 2, DECAY_LORA, RWKV_DIM), 0.1 * DECAY_LORA ** -0.5),
        'a0': nrm((L, RWKV_DIM), 0.1),
        'a_up': nrm((L, AAA_LORA, RWKV_DIM), 0.5 * AAA_LORA ** -0.5),
        'g_up': nrm((L, GATE_LORA, RWKV_DIM), GATE_LORA ** -0.5),
        'k_k': 0.85 + nrm((L, RWKV_DIM), 0.05),
        'k_a': 1.0 + nrm((L, RWKV_DIM), 0.05),
        'r_k': nrm((L, RWKV_HEADS, RWKV_HEAD), 0.1),
        'lnx_g': 1.0 + nrm((L, RWKV_DIM), 0.02), 'lnx_b': nrm((L, RWKV_DIM), 0.02),
        'p_rwkv': nrm((L, RWKV_DIM, D), RWKV_DIM ** -0.5),
        'w_o': nrm((L, D, D), BETA * D ** -0.5),
        'ln2_g': 1.0 + nrm((L, D), 0.02), 'ln2_b': nrm((L, D), 0.02),
        'mem_g': 1.0 + nrm((L, D), 0.02), 'mem_b': nrm((L, D), 0.02),
        'w_cq': nrm((L, D, x_out), D ** -0.5),
        'w_ckv': nrm((L, D, 2 * x_out), D ** -0.5),
        'w_co': nrm((L, x_out, D), BETA * x_out ** -0.5),
        'ln3_g': 1.0 + nrm((L, D), 0.02), 'ln3_b': nrm((L, D), 0.02),
        'ffn2_wgu': nrm((L, D, 2 * D_FF), D ** -0.5),
        'ffn2_wd': nrm((L, D_FF, D), BETA * D_FF ** -0.5),
        'ln4_g': 1.0 + nrm((L, D), 0.02), 'ln4_b': nrm((L, D), 0.02),
    }


def reference(x_prompt, x_sample, mem_prompt, mem_sample, ln1_g, ln1_b, ffn1_wgu, ffn1_wd, w_in, b_gate,
              q_norm_g, w_uq, kv_norm_g, w_ukv, p_mla, mu_prev, mu_next, w0, w_up, a0, a_up, g_up,
              k_k, k_a, r_k, lnx_g, lnx_b, p_rwkv, w_o, ln2_g, ln2_b, mem_g, mem_b, w_cq, w_ckv, w_co,
              ln3_g, ln3_b, ffn2_wgu, ffn2_wd, ln4_g, ln4_b):
    y_prompt, y_sample = x_prompt, x_sample
    for l in range(DEPTH):
        p = dict(ln1_g=ln1_g[l], ln1_b=ln1_b[l], ffn1_wgu=ffn1_wgu[l], ffn1_wd=ffn1_wd[l],
                 w_in=w_in[l], b_gate=b_gate[l], q_norm_g=q_norm_g[l], w_uq=w_uq[l],
                 kv_norm_g=kv_norm_g[l], w_ukv=w_ukv[l], p_mla=p_mla[l], mu_prev=mu_prev[l],
                 mu_next=mu_next[l], w0=w0[l], w_up=w_up[l], a0=a0[l], a_up=a_up[l], g_up=g_up[l],
                 k_k=k_k[l], k_a=k_a[l], r_k=r_k[l], lnx_g=lnx_g[l], lnx_b=lnx_b[l], p_rwkv=p_rwkv[l],
                 w_o=w_o[l], ln2_g=ln2_g[l], ln2_b=ln2_b[l], mem_g=mem_g[l], mem_b=mem_b[l],
                 w_cq=w_cq[l], w_ckv=w_ckv[l], w_co=w_co[l], ln3_g=ln3_g[l], ln3_b=ln3_b[l],
                 ffn2_wgu=ffn2_wgu[l], ffn2_wd=ffn2_wd[l], ln4_g=ln4_g[l], ln4_b=ln4_b[l])
        y_prompt = _layer(y_prompt, mem_prompt, p)
        y_sample = _layer(y_sample, mem_sample, p)
    return (y_prompt, y_sample)
```

```python
import functools

import jax
import jax.numpy as jnp
from jax import lax
from jax.experimental import pallas as pl
from jax.experimental.pallas import tpu as pltpu

F32 = jnp.float32
BF16 = jnp.bfloat16

D_MODEL = 1024
DEPTH = 1
MEM_LEN = 256
MLA_HEADS = 8
Q_LORA = 384
KV_LORA = 256
QK_NOPE = 64
QK_ROPE = 32
V_HEAD = 64
ROPE_THETA = 10000.0
RWKV_HEADS = 8
RWKV_HEAD = 64
RWKV_DIM = RWKV_HEADS * RWKV_HEAD
DECAY_LORA = 64
AAA_LORA = 64
GATE_LORA = 128
X_HEADS = 4
X_HEAD = 128
D_FF = 2816
N_BRANCH = 2
LN_EPS = 1e-5
RMS_EPS = 1e-6
GN_EPS = 64e-5
ALPHA = (2 * DEPTH) ** 0.25

SEG_Q = Q_LORA
SEG_KV = KV_LORA + QK_ROPE
SEG_RWKV = 3 * RWKV_DIM + 2 * DECAY_LORA + AAA_LORA + GATE_LORA
SEG_GATE = N_BRANCH * D_MODEL
OFF_KV = SEG_Q
OFF_RWKV = OFF_KV + SEG_KV
OFF_GATE = OFF_RWKV + SEG_RWKV

LANE = 128
QK_PAD = LANE
RW_PAD = 3 * RWKV_DIM + 3 * LANE
CHUNK = 64
PAIR = 2 * RWKV_HEAD
N_PAIR = RWKV_HEADS // 2
VMEM_LIMIT = 56 * 1024 * 1024


def _dot(a, b):
    return jnp.dot(a, b, preferred_element_type=F32)


def _dot_nt(a, b):
    return lax.dot_general(a, b, (((1,), (1,)), ((), ())), preferred_element_type=F32)


def _dot_tn(a, b):
    return lax.dot_general(a, b, (((0,), (0,)), ((), ())), preferred_element_type=F32)


def _split_bf16(x, parts):
    out = []
    r = x
    for _ in range(parts):
        p = r.astype(BF16)
        out.append(p)
        r = r - p.astype(F32)
    return out


def _dot_precise_r(x, m, parts):
    acc = None
    for p in _split_bf16(x, parts):
        t = _dot(p, m)
        acc = t if acc is None else acc + t
    return acc


def _dot_precise_l(m, x, parts):
    acc = None
    for p in _split_bf16(x, parts):
        t = _dot(m, p)
        acc = t if acc is None else acc + t
    return acc


def _layernorm(y, g, b):
    mu = jnp.mean(y, -1, keepdims=True)
    d = y - mu
    var = jnp.mean(d * d, -1, keepdims=True)
    return d * lax.rsqrt(var + LN_EPS) * g + b


def _rmsnorm(y, g):
    return y * lax.rsqrt(jnp.mean(y * y, -1, keepdims=True) + RMS_EPS) * g


def _sigmoid(x):
    return 1.0 / (1.0 + jnp.exp(-x))


def _params(sem):
    return pltpu.CompilerParams(dimension_semantics=sem, vmem_limit_bytes=VMEM_LIMIT)


def _const_spec(shape):
    nd = len(shape)
    return pl.BlockSpec(shape, lambda *_: (0,) * nd)


def _ffn_ln_kernel(x_ref, wg_ref, wu_ref, wd_ref, g_ref, b_ref, o_ref, acc_ref, *, n_f):
    j = pl.program_id(1)
    x = x_ref[...]
    xb = x.astype(BF16)
    gate = _dot(xb, wg_ref[...])
    up = _dot(xb, wu_ref[...])
    h = (gate * _sigmoid(gate) * up).astype(BF16)
    part = _dot(h, wd_ref[...])

    @pl.when(j == 0)
    def _():
        acc_ref[...] = part

    @pl.when(j > 0)
    def _():
        acc_ref[...] += part

    @pl.when(j == n_f - 1)
    def _():
        y = ALPHA * x + 0.5 * acc_ref[...]
        o_ref[...] = _layernorm(y, g_ref[...], b_ref[...])


def _ffn_ln(x, wg, wu, wd, g, b):
    n, d = x.shape
    tm = min(512, n)
    tf = D_FF // 2
    n_f = D_FF // tf
    return pl.pallas_call(
        functools.partial(_ffn_ln_kernel, n_f=n_f),
        out_shape=jax.ShapeDtypeStruct((n, d), F32),
        grid=(n // tm, n_f),
        in_specs=[
            pl.BlockSpec((tm, d), lambda i, j: (i, 0)),
            pl.BlockSpec((d, tf), lambda i, j: (0, j)),
            pl.BlockSpec((d, tf), lambda i, j: (0, j)),
            pl.BlockSpec((tf, d), lambda i, j: (j, 0)),
            _const_spec((1, d)),
            _const_spec((1, d)),
        ],
        out_specs=pl.BlockSpec((tm, d), lambda i, j: (i, 0)),
        scratch_shapes=[pltpu.VMEM((tm, d), F32)],
        compiler_params=_params(("parallel", "arbitrary")),
        name="ffn_ln",
    )(x, wg, wu, wd, g, b)


def _inproj_kernel(x_ref, cos_ref, sin_ref, wa_ref, qg_ref, cg_ref, wuq_ref, wuk_ref, wuv_ref,
                   wrw_ref, wgt_ref, bgt_ref, q_o, k_o, v_o, rw_o, gate_o):
    xb = x_ref[0].astype(BF16)
    ha = _dot(xb, wa_ref[...])
    hq = ha[:, :Q_LORA]
    hc = ha[:, Q_LORA:Q_LORA + KV_LORA]
    kr_main = ha[:, Q_LORA + KV_LORA:Q_LORA + KV_LORA + LANE]
    kr_rot = ha[:, Q_LORA + KV_LORA + LANE:]
    cos = cos_ref[...]
    sin = sin_ref[...]
    kr = kr_main * cos + kr_rot * sin
    hqn = _rmsnorm(hq, qg_ref[...]).astype(BF16)
    hcn = _rmsnorm(hc, cg_ref[...]).astype(BF16)
    q2 = _dot(hqn, wuq_ref[...])
    k2 = _dot(hcn, wuk_ref[...])
    v2 = _dot(hcn, wuv_ref[...])
    rot_off = MLA_HEADS * QK_PAD
    for h in range(MLA_HEADS):
        lo = h * QK_PAD
        qh = q2[:, lo:lo + QK_PAD] * cos + q2[:, rot_off + lo:rot_off + lo + QK_PAD] * sin
        q_o[0, h] = qh.astype(BF16)
        k_o[0, h] = (k2[:, lo:lo + QK_PAD] + kr).astype(BF16)
        v_o[0, h] = v2[:, h * V_HEAD:(h + 1) * V_HEAD].astype(BF16)
    rw_o[0] = _dot(xb, wrw_ref[...])
    gate_o[0] = _sigmoid(_dot(xb, wgt_ref[...]) + bgt_ref[...]).astype(BF16)


def _inproj(x, cos_t, sin_t, w):
    b, s, d = x.shape
    ts = min(256, s)
    hh = MLA_HEADS
    wl = [w["wa"], w["q_norm_g"], w["kv_norm_g"], w["wuq"], w["wuk"], w["wuv"], w["wrw"], w["wgate"],
          w["b_gate"]]
    return pl.pallas_call(
        _inproj_kernel,
        out_shape=(
            jax.ShapeDtypeStruct((b, hh, s, QK_PAD), BF16),
            jax.ShapeDtypeStruct((b, hh, s, QK_PAD), BF16),
            jax.ShapeDtypeStruct((b, hh, s, V_HEAD), BF16),
            jax.ShapeDtypeStruct((b, s, RW_PAD), F32),
            jax.ShapeDtypeStruct((b, s, SEG_GATE), BF16),
        ),
        grid=(b, s // ts),
        in_specs=[
            pl.BlockSpec((1, ts, d), lambda bi, i: (bi, i, 0)),
            pl.BlockSpec((ts, LANE), lambda bi, i: (i, 0)),
            pl.BlockSpec((ts, LANE), lambda bi, i: (i, 0)),
        ] + [_const_spec(a.shape) for a in wl],
        out_specs=(
            pl.BlockSpec((1, hh, ts, QK_PAD), lambda bi, i: (bi, 0, i, 0)),
            pl.BlockSpec((1, hh, ts, QK_PAD), lambda bi, i: (bi, 0, i, 0)),
            pl.BlockSpec((1, hh, ts, V_HEAD), lambda bi, i: (bi, 0, i, 0)),
            pl.BlockSpec((1, ts, RW_PAD), lambda bi, i: (bi, i, 0)),
            pl.BlockSpec((1, ts, SEG_GATE), lambda bi, i: (bi, i, 0)),
        ),
        compiler_params=_params(("parallel", "parallel")),
        name="in_proj",
    )(x, cos_t, sin_t, *wl)


def _mla_kernel(q_ref, k_ref, v_ref, o_ref, *, tk, n_kv):
    tq = q_ref.shape[2]
    for h in range(MLA_HEADS):
        q = q_ref[0, h]

        def body(j, carry, h=h, q=q):
            m, l, acc = carry
            off = pl.multiple_of(j * tk, tk)
            kj = k_ref[0, h, pl.ds(off, tk), :]
            vj = v_ref[0, h, pl.ds(off, tk), :]
            s = _dot_nt(q, kj)
            m_new = jnp.maximum(m, jnp.max(s, -1, keepdims=True))
            a = jnp.exp(m - m_new)
            p = jnp.exp(s - m_new)
            l = a * l + jnp.sum(p, -1, keepdims=True)
            acc = a * acc + _dot(p.astype(BF16), vj)
            return m_new, l, acc

        init = (jnp.full((tq, 1), -1e30, F32), jnp.zeros((tq, 1), F32), jnp.zeros((tq, V_HEAD), F32))
        m, l, acc = lax.fori_loop(0, n_kv, body, init)
        o_ref[0, :, h * V_HEAD:(h + 1) * V_HEAD] = (acc / l).astype(BF16)


def _mla_attention(q, k, v):
    b, hh, s, _ = q.shape
    tq = min(512, s)
    tk = min(512, s)
    return pl.pallas_call(
        functools.partial(_mla_kernel, tk=tk, n_kv=s // tk),
        out_shape=jax.ShapeDtypeStruct((b, s, hh * V_HEAD), BF16),
        grid=(b, s // tq),
        in_specs=[
            pl.BlockSpec((1, hh, tq, QK_PAD), lambda bi, i: (bi, 0, i, 0)),
            pl.BlockSpec((1, hh, s, QK_PAD), lambda bi, i: (bi, 0, 0, 0)),
            pl.BlockSpec((1, hh, s, V_HEAD), lambda bi, i: (bi, 0, 0, 0)),
        ],
        out_specs=pl.BlockSpec((1, tq, hh * V_HEAD), lambda bi, i: (bi, i, 0)),
        compiler_params=_params(("parallel", "arbitrary")),
        name="mla_attention",
    )(q, k, v)


def _rwkv_prep_kernel(h_ref, hp_ref, hn_ref, mup_ref, mun_ref, w0_ref, wup_ref, a0_ref, aup_ref, gup_ref,
                      kk_ref, ka_ref, rk_ref, hsum_ref, tril_ref, triu_ref,
                      opf_o, opb_o, v_o, pcf_o, pcb_o, bonus_o, g_o):
    i = pl.program_id(1)
    n_i = pl.num_programs(1)
    h = h_ref[0]
    ts = h.shape[0]
    dm = RWKV_DIM
    prow = jnp.where(i > 0, hp_ref[0, 7:8, :], 0.0)
    nrow = jnp.where(i < n_i - 1, hn_ref[0, 0:1, :], 0.0)
    rid = lax.broadcasted_iota(jnp.int32, h.shape, 0)
    prev = jnp.where(rid == 0, prow, pltpu.roll(h, 1, 0))
    nxt = jnp.where(rid == ts - 1, nrow, pltpu.roll(h, ts - 1, 0))
    hs = h + mup_ref[...] * (prev - h) + mun_ref[...] * (nxt - h)
    r = hs[:, :dm]
    k = hs[:, dm:2 * dm]
    v = hs[:, 2 * dm:3 * dm]
    wd = hs[:, 3 * dm:3 * dm + LANE]
    ad = hs[:, 3 * dm + LANE:3 * dm + 2 * LANE]
    gd = hs[:, 3 * dm + 2 * LANE:]

    z = -(w0_ref[...] + _dot(jnp.tanh(wd).astype(BF16), wup_ref[...]))
    softplus = jnp.maximum(z, 0.0) + jnp.log(1.0 + jnp.exp(-jnp.abs(z)))
    lw = -jnp.exp(-softplus - 0.5)
    a_sig = _sigmoid(a0_ref[...] + _dot(ad.astype(BF16), aup_ref[...]))
    g_o[0] = _dot(_sigmoid(gd).astype(BF16), gup_ref[...])

    hsum = hsum_ref[...]
    kk = k * kk_ref[...]
    kk_norm = jnp.sqrt(_dot_precise_r(kk * kk, hsum, 2))
    kk = kk / jnp.maximum(kk_norm, 1e-12)
    k = k * (1.0 + (a_sig - 1.0) * ka_ref[...])
    bonus_o[0] = _dot_precise_r(r * k * rk_ref[...], hsum, 2) * v
    a_vec = -kk
    b_vec = kk * a_sig
    v_o[0] = v.astype(BF16)

    n_c = ts // CHUNK
    for d, (tri_ref, op_o, pc_o) in enumerate(((tril_ref, opf_o, pcf_o), (triu_ref, opb_o, pcb_o))):
        lwd = lw[:, d * dm:(d + 1) * dm]
        cum = _dot_precise_l(tri_ref[...], lwd, 3)
        grow = jnp.exp(cum)
        shrink = jnp.exp(-cum)
        op_o[0, 0] = (r * grow).astype(BF16)
        op_o[0, 1] = (a_vec * jnp.exp(cum - lwd)).astype(BF16)
        op_o[0, 2] = (b_vec * shrink).astype(BF16)
        op_o[0, 3] = (k * shrink).astype(BF16)
        for c in range(n_c):
            row = c * CHUNK + (CHUNK - 1 if d == 0 else 0)
            pc_o[0, c] = grow[row:row + 1, :]


def _rwkv_prep(hrw, w):
    b, s, _ = hrw.shape
    ts = min(256, s)
    n_c = ts // CHUNK
    dm = RWKV_DIM
    rr = jnp.arange(ts)
    same = (rr[:, None] // CHUNK) == (rr[None, :] // CHUNK)
    tril = (same & (rr[None, :] <= rr[:, None])).astype(BF16)
    triu = (same & (rr[None, :] >= rr[:, None])).astype(BF16)
    wl = [w["mu_prev"], w["mu_next"], w["w0"], w["wup"], w["a0"], w["aup"], w["gup"], w["k_k"], w["k_a"],
          w["r_k"], w["hsum"], tril, triu]
    blk8 = ts // 8
    n8 = s // 8
    tok = lambda bi, i: (bi, i, 0)
    return pl.pallas_call(
        _rwkv_prep_kernel,
        out_shape=(
            jax.ShapeDtypeStruct((b, 4, s, dm), BF16),
            jax.ShapeDtypeStruct((b, 4, s, dm), BF16),
            jax.ShapeDtypeStruct((b, s, dm), BF16),
            jax.ShapeDtypeStruct((b, s // CHUNK, 1, dm), F32),
            jax.ShapeDtypeStruct((b, s // CHUNK, 1, dm), F32),
            jax.ShapeDtypeStruct((b, s, dm), F32),
            jax.ShapeDtypeStruct((b, s, dm), F32),
        ),
        grid=(b, s // ts),
        in_specs=[
            pl.BlockSpec((1, ts, RW_PAD), tok),
            pl.BlockSpec((1, 8, RW_PAD), lambda bi, i: (bi, jnp.maximum(i * blk8 - 1, 0), 0)),
            pl.BlockSpec((1, 8, RW_PAD), lambda bi, i: (bi, jnp.minimum((i + 1) * blk8, n8 - 1), 0)),
        ] + [_const_spec(a.shape) for a in wl],
        out_specs=(
            pl.BlockSpec((1, 4, ts, dm), lambda bi, i: (bi, 0, i, 0)),
            pl.BlockSpec((1, 4, ts, dm), lambda bi, i: (bi, 0, i, 0)),
            pl.BlockSpec((1, ts, dm), tok),
            pl.BlockSpec((1, n_c, 1, dm), lambda bi, i: (bi, i, 0, 0)),
            pl.BlockSpec((1, n_c, 1, dm), lambda bi, i: (bi, i, 0, 0)),
            pl.BlockSpec((1, ts, dm), tok),
            pl.BlockSpec((1, ts, dm), tok),
        ),
        compiler_params=_params(("parallel", "parallel")),
        name="rwkv_prep",
    )(hrw, hrw, hrw, *wl)


def _block_diag(x, blockmask):
    return jnp.where(blockmask, jnp.concatenate([x, x], 0), jnp.zeros((), x.dtype))


def _scan_chunk(at, rt, bt, kt, v, pc, hstate, strict, incl, eye2, blockmask):
    bd = lambda x: _block_diag(x, blockmask)
    bdv = bd(v)
    a4 = _dot_nt(jnp.concatenate([at, rt], 0), jnp.concatenate([bd(bt), bd(kt)], 0))
    a_ab = jnp.where(strict, a4[:CHUNK, :PAIR], 0.0)
    a_ak = jnp.where(strict, a4[:CHUNK, PAIR:], 0.0)
    a_rb = jnp.where(incl, a4[CHUNK:, :PAIR], 0.0)
    a_rk = jnp.where(incl, a4[CHUNK:, PAIR:], 0.0)
    w1 = _dot(a_ak.astype(BF16), bdv)
    x = eye2 + a_ab
    ab = a_ab.astype(BF16)
    ak = _dot(ab, bd(ab))
    for lvl in range(1, 6):
        akb = ak.astype(BF16)
        if lvl < 5:
            res = _dot(akb, jnp.concatenate([bd(x.astype(BF16)), bd(akb)], 1))
            x = x + res[:, :PAIR]
            ak = res[:, PAIR:]
        else:
            x = x + _dot(akb, bd(x.astype(BF16)))
    res = _dot(x.astype(BF16), jnp.concatenate([bd(at), bd(w1.astype(BF16))], 1))
    ap = res[:, :PAIR].astype(BF16)
    vp = res[:, PAIR:].astype(BF16)
    zero = jnp.zeros((PAIR, PAIR), BF16)
    rhs = jnp.concatenate([jnp.concatenate([bd(ap), bd(vp)], 1), jnp.concatenate([zero, bdv], 1)], 0)
    res = _dot(jnp.concatenate([a_rb, a_rk], 1).astype(BF16), rhs)
    rp = rt.astype(F32) + res[:, :PAIR]
    y0 = res[:, PAIR:]
    bh = (bt.astype(F32) * pc).astype(BF16)
    kh = (kt.astype(F32) * pc).astype(BF16)
    lmat = jnp.concatenate([jnp.concatenate([ap, vp], 1),
                            jnp.concatenate([jnp.zeros((CHUNK, PAIR), BF16), v], 1)], 0)
    res = _dot_tn(jnp.concatenate([bh, kh], 0), lmat)
    mb = jnp.where(blockmask, res[:, :PAIR], 0.0)
    gb = jnp.where(blockmask, res[:, PAIR:], 0.0)
    hb = hstate.astype(BF16)
    y = _dot(rp.astype(BF16), hb) + y0
    pcol = jnp.transpose(jnp.broadcast_to(pc, (PAIR, PAIR)))
    h_new = pcol * hstate + _dot(mb.astype(BF16), hb) + gb
    return y, h_new


def _rwkv_scan_kernel(opf_ref, opb_ref, vf_ref, vb_ref, pcf_ref, pcb_ref, yf_o, yb_o, h_ref, *, cpb):
    i = pl.program_id(1)

    @pl.when(i == 0)
    def _():
        h_ref[...] = jnp.zeros_like(h_ref)

    t_id = lax.broadcasted_iota(jnp.int32, (CHUNK, PAIR), 0)
    s_id = lax.broadcasted_iota(jnp.int32, (CHUNK, PAIR), 1) % CHUNK
    eye2 = (t_id == s_id).astype(F32)
    rb = lax.broadcasted_iota(jnp.int32, (PAIR, PAIR), 0) // RWKV_HEAD
    cb = lax.broadcasted_iota(jnp.int32, (PAIR, PAIR), 1) // RWKV_HEAD
    blockmask = rb == cb
    for d, (op_ref, v_ref, pc_ref, y_o) in enumerate(((opf_ref, vf_ref, pcf_ref, yf_o),
                                                       (opb_ref, vb_ref, pcb_ref, yb_o))):
        strict = (s_id < t_id) if d == 0 else (s_id > t_id)
        incl = (s_id <= t_id) if d == 0 else (s_id >= t_id)
        order = range(cpb) if d == 0 else range(cpb - 1, -1, -1)
        for g in range(N_PAIR):
            ln = slice(g * PAIR, (g + 1) * PAIR)
            hstate = h_ref[d, g]
            for c in order:
                rows = slice(c * CHUNK, (c + 1) * CHUNK)
                y, hstate = _scan_chunk(op_ref[0, 1, rows, ln], op_ref[0, 0, rows, ln], op_ref[0, 2, rows, ln],
                                        op_ref[0, 3, rows, ln], v_ref[0, rows, ln], pc_ref[0, c, :, ln], hstate,
                                        strict, incl, eye2, blockmask)
                y_o[0, rows, ln] = y
            h_ref[d, g] = hstate


def _rwkv_scan(opf, opb, v, pcf, pcb):
    b, _, s, dm = opf.shape
    cpb = 1
    cb = cpb * CHUNK
    n_b = s // cb
    fw = lambda bi, i: (bi, i, 0)
    bw = lambda bi, i: (bi, n_b - 1 - i, 0)
    return pl.pallas_call(
        functools.partial(_rwkv_scan_kernel, cpb=cpb),
        out_shape=(jax.ShapeDtypeStruct((b, s, dm), F32), jax.ShapeDtypeStruct((b, s, dm), F32)),
        grid=(b, n_b),
        in_specs=[
            pl.BlockSpec((1, 4, cb, dm), lambda bi, i: (bi, 0, i, 0)),
            pl.BlockSpec((1, 4, cb, dm), lambda bi, i: (bi, 0, n_b - 1 - i, 0)),
            pl.BlockSpec((1, cb, dm), fw),
            pl.BlockSpec((1, cb, dm), bw),
            pl.BlockSpec((1, cpb, 1, dm), lambda bi, i: (bi, i, 0, 0)),
            pl.BlockSpec((1, cpb, 1, dm), lambda bi, i: (bi, n_b - 1 - i, 0, 0)),
        ],
        out_specs=(pl.BlockSpec((1, cb, dm), fw), pl.BlockSpec((1, cb, dm), bw)),
        scratch_shapes=[pltpu.VMEM((2, N_PAIR, PAIR, PAIR), F32)],
        compiler_params=_params(("parallel", "arbitrary")),
        name="rwkv_scan",
    )(opf, opb, v, v, pcf, pcb)


def _mem_kv_kernel(m_ref, g_ref, b_ref, w_ref, k_o, v_o):
    m = _layernorm(m_ref[0], g_ref[...], b_ref[...]).astype(BF16)
    kv = _dot(m, w_ref[...])
    half = X_HEADS * X_HEAD
    for h in range(X_HEADS):
        k_o[0, h] = kv[:, h * X_HEAD:(h + 1) * X_HEAD].astype(BF16)
        v_o[0, h] = kv[:, half + h * X_HEAD:half + (h + 1) * X_HEAD].astype(BF16)


def _mem_kv(mem, w):
    b, ml, d = mem.shape
    shp = jax.ShapeDtypeStruct((b, X_HEADS, ml, X_HEAD), BF16)
    spec = pl.BlockSpec((1, X_HEADS, ml, X_HEAD), lambda bi: (bi, 0, 0, 0))
    return pl.pallas_call(
        _mem_kv_kernel,
        out_shape=(shp, shp),
        grid=(b,),
        in_specs=[pl.BlockSpec((1, ml, d), lambda bi: (bi, 0, 0)), _const_spec((1, d)), _const_spec((1, d)),
                  _const_spec(w["w_ckv"].shape)],
        out_specs=(spec, spec),
        compiler_params=_params(("parallel",)),
        name="mem_kv",
    )(mem, w["mem_g"], w["mem_b"], w["w_ckv"])


def _mix_cross_kernel(x_ref, o_ref, yf_ref, yb_ref, bonus_ref, g_ref, gate_ref, mk_ref, mv_ref,
                      lnxg_ref, lnxb_ref, havg_ref, pmla_ref, prw_ref, wo_ref, ln2g_ref, ln2b_ref,
                      wcq_ref, wco_ref, ln3g_ref, ln3b_ref, out_ref):
    x = x_ref[0]
    havg = havg_ref[...]
    y = yf_ref[0] + yb_ref[0]
    mu = _dot_precise_r(y, havg, 2)
    dlt = y - mu
    var = _dot_precise_r(dlt * dlt, havg, 2)
    yn = dlt * lax.rsqrt(var + GN_EPS) * lnxg_ref[...] + lnxb_ref[...]
    rw = ((yn + bonus_ref[0]) * g_ref[0]).astype(BF16)
    br_b = _dot(rw, prw_ref[...])
    br_a = _dot(o_ref[0], pmla_ref[...])
    gates = gate_ref[0].astype(F32)
    mixin = (gates[:, :D_MODEL] * br_a + gates[:, D_MODEL:] * br_b).astype(BF16)
    x2 = _layernorm(ALPHA * x + _dot(mixin, wo_ref[...]), ln2g_ref[...], ln2b_ref[...])

    q = _dot(x2.astype(BF16), wcq_ref[...]).astype(BF16)
    heads = []
    for h in range(X_HEADS):
        s = _dot_nt(q[:, h * X_HEAD:(h + 1) * X_HEAD], mk_ref[0, h])
        p = jnp.exp(s - jnp.max(s, -1, keepdims=True))
        oh = _dot(p.astype(BF16), mv_ref[0, h]) / jnp.sum(p, -1, keepdims=True)
        heads.append(oh.astype(BF16))
    cross = _dot(jnp.concatenate(heads, 1), wco_ref[...])
    out_ref[0] = _layernorm(ALPHA * x2 + cross, ln3g_ref[...], ln3b_ref[...])


def _mix_cross(x1, o, yf, yb, bonus, g, gates, mk, mv, w):
    b, s, d = x1.shape
    ts = min(256, s)
    dm = RWKV_DIM
    ml = mk.shape[2]
    wl = [w["lnx_g"], w["lnx_b"], w["havg"], w["p_mla"], w["p_rwkv"], w["w_o"], w["ln2_g"], w["ln2_b"],
          w["w_cq"], w["w_co"], w["ln3_g"], w["ln3_b"]]
    tok = lambda bi, i: (bi, i, 0)
    mspec = pl.BlockSpec((1, X_HEADS, ml, X_HEAD), lambda bi, i: (bi, 0, 0, 0))
    return pl.pallas_call(
        _mix_cross_kernel,
        out_shape=jax.ShapeDtypeStruct((b, s, d), F32),
        grid=(b, s // ts),
        in_specs=[
            pl.BlockSpec((1, ts, d), tok),
            pl.BlockSpec((1, ts, MLA_HEADS * V_HEAD), tok),
            pl.BlockSpec((1, ts, dm), tok),
            pl.BlockSpec((1, ts, dm), tok),
            pl.BlockSpec((1, ts, dm), tok),
            pl.BlockSpec((1, ts, dm), tok),
            pl.BlockSpec((1, ts, SEG_GATE), tok),
            mspec, mspec,
        ] + [_const_spec(a.shape) for a in wl],
        out_specs=pl.BlockSpec((1, ts, d), tok),
        compiler_params=_params(("parallel", "parallel")),
        name="mix_cross",
    )(x1, o, yf, yb, bonus, g, gates, mk, mv, *wl)


def _rope_tables(s):
    inv = 1.0 / (ROPE_THETA ** (jnp.arange(0, QK_ROPE, 2, dtype=F32) / QK_ROPE))
    ang = jnp.arange(s, dtype=F32)[:, None] * inv[None, :]
    cos, sin = jnp.cos(ang), jnp.sin(ang)
    half = QK_ROPE // 2
    cos_t = jnp.ones((s, LANE), F32).at[:, QK_NOPE:QK_NOPE + QK_ROPE].set(jnp.concatenate([cos, cos], 1))
    sin_t = jnp.zeros((s, LANE), F32).at[:, QK_NOPE:QK_NOPE + QK_ROPE].set(jnp.concatenate([sin, sin], 1))
    del half
    return cos_t, sin_t


def _rot_cols(w):
    half = QK_ROPE // 2
    return jnp.concatenate([-w[..., half:], w[..., :half]], -1)


def _prep_weights(p):
    row = lambda a: a.reshape(1, -1).astype(F32)
    bf = lambda a: a.astype(BF16)
    w = {}
    for name in ("ln1", "ln2", "ln3", "ln4"):
        w[name + "_g"] = row(p[name + "_g"])
        w[name + "_b"] = row(p[name + "_b"])
    for name in ("ffn1", "ffn2"):
        wgu = p[name + "_wgu"]
        w[name + "_wg"] = bf(wgu[:, :D_FF])
        w[name + "_wu"] = bf(wgu[:, D_FF:])
        w[name + "_wd"] = bf(p[name + "_wd"])

    w_in = p["w_in"]
    d = w_in.shape[0]
    w_kr = w_in[:, OFF_KV + KV_LORA:OFF_RWKV]
    pad_l = jnp.zeros((d, QK_NOPE), F32)
    pad_r = jnp.zeros((d, LANE - QK_NOPE - QK_ROPE), F32)
    w["wa"] = bf(jnp.concatenate([w_in[:, :OFF_KV], w_in[:, OFF_KV:OFF_KV + KV_LORA],
                                  pad_l, w_kr, pad_r, pad_l, _rot_cols(w_kr), pad_r], 1))
    w["q_norm_g"] = row(p["q_norm_g"])
    w["kv_norm_g"] = row(p["kv_norm_g"])
    scale = (QK_NOPE + QK_ROPE) ** -0.5
    wuq = (p["w_uq"] * scale).reshape(Q_LORA, MLA_HEADS, QK_NOPE + QK_ROPE)
    zpad = jnp.zeros((Q_LORA, MLA_HEADS, LANE - QK_NOPE - QK_ROPE), F32)
    wuq_main = jnp.concatenate([wuq, zpad], -1)
    wuq_rot = jnp.concatenate([jnp.zeros((Q_LORA, MLA_HEADS, QK_NOPE), F32), _rot_cols(wuq[..., QK_NOPE:]), zpad], -1)
    w["wuq"] = bf(jnp.concatenate([wuq_main.reshape(Q_LORA, -1), wuq_rot.reshape(Q_LORA, -1)], 1))
    wukv = p["w_ukv"].reshape(KV_LORA, MLA_HEADS, QK_NOPE + V_HEAD)
    w["wuk"] = bf(jnp.concatenate([wukv[..., :QK_NOPE], jnp.zeros((KV_LORA, MLA_HEADS, LANE - QK_NOPE), F32)],
                                  -1).reshape(KV_LORA, -1))
    w["wuv"] = bf(wukv[..., QK_NOPE:].reshape(KV_LORA, -1))

    w_rw = w_in[:, OFF_RWKV:OFF_GATE]
    r_wd = 3 * RWKV_DIM
    r_ad = r_wd + 2 * DECAY_LORA
    r_gd = r_ad + AAA_LORA

    def regroup(a):
        z = jnp.zeros(a.shape[:-1] + (LANE - AAA_LORA,), a.dtype)
        return jnp.concatenate([a[..., :r_ad], a[..., r_ad:r_gd], z, a[..., r_gd:]], -1)

    w["wrw"] = bf(regroup(w_rw))
    w["mu_prev"] = row(regroup(p["mu_prev"]))
    w["mu_next"] = row(regroup(p["mu_next"]))
    w["wgate"] = bf(w_in[:, OFF_GATE:])
    w["b_gate"] = row(p["b_gate"])
    w["w0"] = row(p["w0"])
    zup = jnp.zeros((DECAY_LORA, RWKV_DIM), F32)
    w["wup"] = bf(jnp.concatenate([jnp.concatenate([p["w_up"][0], zup], 1),
                                   jnp.concatenate([zup, p["w_up"][1]], 1)], 0))
    w["a0"] = row(p["a0"])
    w["aup"] = bf(jnp.concatenate([p["a_up"], jnp.zeros((LANE - AAA_LORA, RWKV_DIM), F32)], 0))
    w["gup"] = bf(p["g_up"])
    w["k_k"] = row(p["k_k"])
    w["k_a"] = row(p["k_a"])
    w["r_k"] = row(p["r_k"])
    hid = jnp.arange(RWKV_DIM) // RWKV_HEAD
    same_head = (hid[:, None] == hid[None, :]).astype(F32)
    w["hsum"] = bf(same_head)
    w["havg"] = bf(same_head / RWKV_HEAD)
    w["lnx_g"] = row(p["lnx_g"])
    w["lnx_b"] = row(p["lnx_b"])
    w["p_mla"] = bf(p["p_mla"])
    w["p_rwkv"] = bf(p["p_rwkv"])
    w["w_o"] = bf(p["w_o"])
    w["mem_g"] = row(p["mem_g"])
    w["mem_b"] = row(p["mem_b"])
    w["w_cq"] = bf(p["w_cq"] * (X_HEAD ** -0.5))
    w["w_ckv"] = bf(p["w_ckv"])
    w["w_co"] = bf(p["w_co"])
    return w


def _layer(x, mem, w):
    b, s, d = x.shape
    cos_t, sin_t = _rope_tables(s)
    x1 = _ffn_ln(x.reshape(b * s, d), w["ffn1_wg"], w["ffn1_wu"], w["ffn1_wd"], w["ln1_g"], w["ln1_b"])
    x1 = x1.reshape(b, s, d)
    q, k, v, hrw, gates = _inproj(x1, cos_t, sin_t, w)
    o = _mla_attention(q, k, v)
    opf, opb, vr, pcf, pcb, bonus, g = _rwkv_prep(hrw, w)
    yf, yb = _rwkv_scan(opf, opb, vr, pcf, pcb)
    mk, mv = _mem_kv(mem, w)
    x3 = _mix_cross(x1, o, yf, yb, bonus, g, gates, mk, mv, w)
    out = _ffn_ln(x3.reshape(b * s, d), w["ffn2_wg"], w["ffn2_wu"], w["ffn2_wd"], w["ln4_g"], w["ln4_b"])
    return out.reshape(b, s, d)


_PARAM_NAMES = ("ln1_g", "ln1_b", "ffn1_wgu", "ffn1_wd", "w_in", "b_gate", "q_norm_g", "w_uq", "kv_norm_g",
                "w_ukv", "p_mla", "mu_prev", "mu_next", "w0", "w_up", "a0", "a_up", "g_up", "k_k", "k_a", "r_k",
                "lnx_g", "lnx_b", "p_rwkv", "w_o", "ln2_g", "ln2_b", "mem_g", "mem_b", "w_cq", "w_ckv", "w_co",
                "ln3_g", "ln3_b", "ffn2_wgu", "ffn2_wd", "ln4_g", "ln4_b")


def kernel(x_prompt, x_sample, mem_prompt, mem_sample, ln1_g, ln1_b, ffn1_wgu, ffn1_wd, w_in, b_gate, q_norm_g, w_uq, kv_norm_g, w_ukv, p_mla, mu_prev, mu_next, w0, w_up, a0, a_up, g_up, k_k, k_a, r_k, lnx_g, lnx_b, p_rwkv, w_o, ln2_g, ln2_b, mem_g, mem_b, w_cq, w_ckv, w_co, ln3_g, ln3_b, ffn2_wgu, ffn2_wd, ln4_g, ln4_b):
    stacked = dict(zip(_PARAM_NAMES, (ln1_g, ln1_b, ffn1_wgu, ffn1_wd, w_in, b_gate, q_norm_g, w_uq, kv_norm_g,
                                      w_ukv, p_mla, mu_prev, mu_next, w0, w_up, a0, a_up, g_up, k_k, k_a, r_k,
                                      lnx_g, lnx_b, p_rwkv, w_o, ln2_g, ln2_b, mem_g, mem_b, w_cq, w_ckv, w_co,
                                      ln3_g, ln3_b, ffn2_wgu, ffn2_wd, ln4_g, ln4_b)))
    y_prompt, y_sample = x_prompt, x_sample
    for layer in range(DEPTH):
        w = _prep_weights({name: a[layer] for name, a in stacked.items()})
        y_prompt = _layer(y_prompt, mem_prompt, w)
        y_sample = _layer(y_sample, mem_sample, w)
    return (y_prompt, y_sample)
```

```python
import functools

import jax
import jax.numpy as jnp
from jax import lax
from jax.experimental import pallas as pl
from jax.experimental.pallas import tpu as pltpu

F32 = jnp.float32
BF16 = jnp.bfloat16

D_MODEL = 1024
DEPTH = 1
MEM_LEN = 256
MLA_HEADS = 8
Q_LORA = 384
KV_LORA = 256
QK_NOPE = 64
QK_ROPE = 32
V_HEAD = 64
ROPE_THETA = 10000.0
RWKV_HEADS = 8
RWKV_HEAD = 64
RWKV_DIM = RWKV_HEADS * RWKV_HEAD
DECAY_LORA = 64
AAA_LORA = 64
GATE_LORA = 128
X_HEADS = 4
X_HEAD = 128
D_FF = 2816
N_BRANCH = 2
LN_EPS = 1e-5
RMS_EPS = 1e-6
GN_EPS = 64e-5
ALPHA = (2 * DEPTH) ** 0.25

SEG_Q = Q_LORA
SEG_KV = KV_LORA + QK_ROPE
SEG_RWKV = 3 * RWKV_DIM + 2 * DECAY_LORA + AAA_LORA + GATE_LORA
SEG_GATE = N_BRANCH * D_MODEL
OFF_KV = SEG_Q
OFF_RWKV = OFF_KV + SEG_KV
OFF_GATE = OFF_RWKV + SEG_RWKV

LANE = 128
QK_PAD = LANE
V_ROWS = V_HEAD + 16
ATTN_STREAMS = 4
LOG2E = 1.4426950408889634
RW_PAD = 3 * RWKV_DIM + 3 * LANE
CHUNK = 64
PAIR = 2 * RWKV_HEAD
N_PAIR = RWKV_HEADS // 2
VMEM_LIMIT = 56 * 1024 * 1024


def _dot(a, b):
    return jnp.dot(a, b, preferred_element_type=F32)


def _dot_nt(a, b):
    return lax.dot_general(a, b, (((1,), (1,)), ((), ())), preferred_element_type=F32)


def _dot_tn(a, b):
    return lax.dot_general(a, b, (((0,), (0,)), ((), ())), preferred_element_type=F32)


def _split_bf16(x, parts):
    out = []
    r = x
    for _ in range(parts):
        p = r.astype(BF16)
        out.append(p)
        r = r - p.astype(F32)
    return out


def _dot_precise_r(x, m, parts):
    acc = None
    for p in _split_bf16(x, parts):
        t = _dot(p, m)
        acc = t if acc is None else acc + t
    return acc


def _dot_precise_l(m, x, parts):
    acc = None
    for p in _split_bf16(x, parts):
        t = _dot(m, p)
        acc = t if acc is None else acc + t
    return acc


def _layernorm(y, g, b):
    mu = jnp.mean(y, -1, keepdims=True)
    d = y - mu
    var = jnp.mean(d * d, -1, keepdims=True)
    return d * lax.rsqrt(var + LN_EPS) * g + b


def _rmsnorm(y, g):
    return y * lax.rsqrt(jnp.mean(y * y, -1, keepdims=True) + RMS_EPS) * g


def _sigmoid(x):
    return 1.0 / (1.0 + jnp.exp(-x))


def _params(sem):
    return pltpu.CompilerParams(dimension_semantics=sem, vmem_limit_bytes=VMEM_LIMIT)


def _const_spec(shape):
    nd = len(shape)
    return pl.BlockSpec(shape, lambda *_: (0,) * nd)


def _ffn_ln_kernel(x_ref, wg_ref, wu_ref, wd_ref, g_ref, b_ref, o_ref, acc_ref, *, n_f):
    j = pl.program_id(1)
    x = x_ref[...]
    xb = x.astype(BF16)
    gate = _dot(xb, wg_ref[...])
    up = _dot(xb, wu_ref[...])
    h = (gate * _sigmoid(gate) * up).astype(BF16)
    part = _dot(h, wd_ref[...])

    @pl.when(j == 0)
    def _():
        acc_ref[...] = part

    @pl.when(j > 0)
    def _():
        acc_ref[...] += part

    @pl.when(j == n_f - 1)
    def _():
        y = ALPHA * x + 0.5 * acc_ref[...]
        o_ref[...] = _layernorm(y, g_ref[...], b_ref[...])


def _ffn_ln(x, wg, wu, wd, g, b):
    n, d = x.shape
    tm = min(512, n)
    tf = D_FF // 2
    n_f = D_FF // tf
    return pl.pallas_call(
        functools.partial(_ffn_ln_kernel, n_f=n_f),
        out_shape=jax.ShapeDtypeStruct((n, d), F32),
        grid=(n // tm, n_f),
        in_specs=[
            pl.BlockSpec((tm, d), lambda i, j: (i, 0)),
            pl.BlockSpec((d, tf), lambda i, j: (0, j)),
            pl.BlockSpec((d, tf), lambda i, j: (0, j)),
            pl.BlockSpec((tf, d), lambda i, j: (j, 0)),
            _const_spec((1, d)),
            _const_spec((1, d)),
        ],
        out_specs=pl.BlockSpec((tm, d), lambda i, j: (i, 0)),
        scratch_shapes=[pltpu.VMEM((tm, d), F32)],
        compiler_params=_params(("parallel", "arbitrary")),
        name="ffn_ln",
    )(x, wg, wu, wd, g, b)


def _inproj_kernel(x_ref, cos_ref, sin_ref, cost_ref, sint_ref, wa_ref, qg_ref, cg_ref, wuqt_ref, wuk_ref,
                   wuvt_ref, wrw_ref, wgt_ref, bgt_ref, qt_o, k_o, vt_o, rw_o, gate_o):
    xb = x_ref[0].astype(BF16)
    ha = _dot(xb, wa_ref[...])
    hq = ha[:, :Q_LORA]
    hc = ha[:, Q_LORA:Q_LORA + KV_LORA]
    kr_main = ha[:, Q_LORA + KV_LORA:Q_LORA + KV_LORA + LANE]
    kr_rot = ha[:, Q_LORA + KV_LORA + LANE:]
    kr = kr_main * cos_ref[...] + kr_rot * sin_ref[...]
    hqn = _rmsnorm(hq, qg_ref[...]).astype(BF16)
    hcn = _rmsnorm(hc, cg_ref[...]).astype(BF16)
    q2t = _dot_nt(wuqt_ref[...], hqn)
    k2 = _dot(hcn, wuk_ref[...])
    v2t = _dot_nt(wuvt_ref[...], hcn)
    cos_c = cost_ref[...]
    sin_c = sint_ref[...]
    rot_off = MLA_HEADS * QK_PAD
    for h in range(MLA_HEADS):
        lo = h * QK_PAD
        qh = q2t[lo:lo + QK_PAD] * cos_c + q2t[rot_off + lo:rot_off + lo + QK_PAD] * sin_c
        qt_o[0, h] = qh.astype(BF16)
        k_o[0, h] = (k2[:, lo:lo + QK_PAD] + kr).astype(BF16)
        vt_o[0, h, 0, :V_HEAD] = v2t[h * V_HEAD:(h + 1) * V_HEAD].astype(BF16)
        vt_o[0, h, 0, V_HEAD:] = jnp.ones((V_ROWS - V_HEAD, v2t.shape[1]), BF16)
    rw_o[0] = _dot(xb, wrw_ref[...])
    gate_o[0] = _sigmoid(_dot(xb, wgt_ref[...]) + bgt_ref[...]).astype(BF16)


def _attn_key_chunk(s):
    return min(512, s)


def _inproj(x, cos_t, sin_t, w):
    b, s, d = x.shape
    ts = min(256, s)
    tk = _attn_key_chunk(s)
    per = tk // ts
    hh = MLA_HEADS
    wl = [w["wa"], w["q_norm_g"], w["kv_norm_g"], w["wuqt"], w["wuk"], w["wuvt"], w["wrw"], w["wgate"],
          w["b_gate"]]
    return pl.pallas_call(
        _inproj_kernel,
        out_shape=(
            jax.ShapeDtypeStruct((b, hh, QK_PAD, s), BF16),
            jax.ShapeDtypeStruct((b, hh, s, QK_PAD), BF16),
            jax.ShapeDtypeStruct((b, hh, s // tk, V_ROWS, tk), BF16),
            jax.ShapeDtypeStruct((b, s, RW_PAD), F32),
            jax.ShapeDtypeStruct((b, s, SEG_GATE), BF16),
        ),
        grid=(b, s // ts),
        in_specs=[
            pl.BlockSpec((1, ts, d), lambda bi, i: (bi, i, 0)),
            pl.BlockSpec((ts, LANE), lambda bi, i: (i, 0)),
            pl.BlockSpec((ts, LANE), lambda bi, i: (i, 0)),
            pl.BlockSpec((LANE, ts), lambda bi, i: (0, i)),
            pl.BlockSpec((LANE, ts), lambda bi, i: (0, i)),
        ] + [_const_spec(a.shape) for a in wl],
        out_specs=(
            pl.BlockSpec((1, hh, QK_PAD, ts), lambda bi, i: (bi, 0, 0, i)),
            pl.BlockSpec((1, hh, ts, QK_PAD), lambda bi, i: (bi, 0, i, 0)),
            pl.BlockSpec((1, hh, 1, V_ROWS, ts), lambda bi, i: (bi, 0, i // per, 0, i % per)),
            pl.BlockSpec((1, ts, RW_PAD), lambda bi, i: (bi, i, 0)),
            pl.BlockSpec((1, ts, SEG_GATE), lambda bi, i: (bi, i, 0)),
        ),
        compiler_params=_params(("parallel", "parallel")),
        name="in_proj",
    )(x, cos_t, sin_t, cos_t.T, sin_t.T, *wl)


def _mla_kernel(qt_ref, k_ref, vt_ref, o_ref, s_ref, *, tk, n_kv):
    tq = qt_ref.shape[3]
    ns = ATTN_STREAMS
    n_t = (MLA_HEADS // ns) * n_kv

    def issue_scores(t, slot):
        g = lax.div(t, n_kv)
        off = pl.multiple_of(lax.rem(t, n_kv) * tk, tk)
        for i in range(ns):
            h = g * ns + i
            s_ref[slot, i] = _dot(k_ref[0, h, pl.ds(off, tk), :], qt_ref[0, h])

    def step(t, slot, carry):
        m, acc = carry[:ns], carry[ns:]
        issue_scores(jnp.minimum(t + 1, n_t - 1), 1 - slot)
        g = lax.div(t, n_kv)
        j = lax.rem(t, n_kv)
        first = j == 0
        m = [jnp.where(first, -1e30, x) for x in m]
        acc = [jnp.where(first, 0.0, x) for x in acc]
        m_new = [jnp.maximum(m[i], jnp.max(s_ref[slot, i], 0, keepdims=True)) for i in range(ns)]
        a = [jnp.exp2(m[i] - m_new[i]) for i in range(ns)]
        p = [jnp.exp2(s_ref[slot, i] - m_new[i]).astype(BF16) for i in range(ns)]
        pv = [_dot(vt_ref[0, g * ns + i, j], p[i]) for i in range(ns)]
        acc = [a[i] * acc[i] + pv[i] for i in range(ns)]

        for i in range(ns):
            row = pl.multiple_of((g * ns + i) * V_HEAD, V_HEAD)
            o_ref[0, pl.ds(row, V_HEAD), :] = (acc[i][:V_HEAD] / acc[i][V_HEAD:V_HEAD + 1]).astype(BF16)
        return tuple(m_new) + tuple(acc)

    issue_scores(0, 0)

    def body(tt, carry):
        carry = step(2 * tt, 0, carry)
        return step(2 * tt + 1, 1, carry)

    init = tuple(jnp.full((1, tq), -1e30, F32) for _ in range(ns)) + \
        tuple(jnp.zeros((V_ROWS, tq), F32) for _ in range(ns))
    lax.fori_loop(0, n_t // 2, body, init)


def _mla_attention(qt, k, vt):
    b, hh, _, s = qt.shape
    tq = min(256, s)
    tk = _attn_key_chunk(s)
    n_kv = s // tk
    return pl.pallas_call(
        functools.partial(_mla_kernel, tk=tk, n_kv=n_kv),
        out_shape=jax.ShapeDtypeStruct((b, hh * V_HEAD, s), BF16),
        grid=(b, s // tq),
        in_specs=[
            pl.BlockSpec((1, hh, QK_PAD, tq), lambda bi, i: (bi, 0, 0, i)),
            pl.BlockSpec((1, hh, s, QK_PAD), lambda bi, i: (bi, 0, 0, 0)),
            pl.BlockSpec((1, hh, n_kv, V_ROWS, tk), lambda bi, i: (bi, 0, 0, 0, 0)),
        ],
        out_specs=pl.BlockSpec((1, hh * V_HEAD, tq), lambda bi, i: (bi, 0, i)),
        scratch_shapes=[pltpu.VMEM((2, ATTN_STREAMS, tk, tq), F32)],
        compiler_params=_params(("parallel", "arbitrary")),
        name="mla_attention",
    )(qt, k, vt)


def _rwkv_prep_kernel(h_ref, hp_ref, hn_ref, mup_ref, mun_ref, w0_ref, wup_ref, a0_ref, aup_ref, gup_ref,
                      kk_ref, ka_ref, rk_ref, hsum_ref, tril_ref, triu_ref,
                      opf_o, opb_o, v_o, pcf_o, pcb_o, bonus_o, g_o):
    i = pl.program_id(1)
    n_i = pl.num_programs(1)
    h = h_ref[0]
    ts = h.shape[0]
    dm = RWKV_DIM
    prow = jnp.where(i > 0, hp_ref[0, 7:8, :], 0.0)
    nrow = jnp.where(i < n_i - 1, hn_ref[0, 0:1, :], 0.0)
    rid = lax.broadcasted_iota(jnp.int32, h.shape, 0)
    prev = jnp.where(rid == 0, prow, pltpu.roll(h, 1, 0))
    nxt = jnp.where(rid == ts - 1, nrow, pltpu.roll(h, ts - 1, 0))
    hs = h + mup_ref[...] * (prev - h) + mun_ref[...] * (nxt - h)
    r = hs[:, :dm]
    k = hs[:, dm:2 * dm]
    v = hs[:, 2 * dm:3 * dm]
    wd = hs[:, 3 * dm:3 * dm + LANE]
    ad = hs[:, 3 * dm + LANE:3 * dm + 2 * LANE]
    gd = hs[:, 3 * dm + 2 * LANE:]

    z = -(w0_ref[...] + _dot(jnp.tanh(wd).astype(BF16), wup_ref[...]))
    softplus = jnp.maximum(z, 0.0) + jnp.log(1.0 + jnp.exp(-jnp.abs(z)))
    lw = -jnp.exp(-softplus - 0.5)
    a_sig = _sigmoid(a0_ref[...] + _dot(ad.astype(BF16), aup_ref[...]))
    g_o[0] = _dot(_sigmoid(gd).astype(BF16), gup_ref[...])

    hsum = hsum_ref[...]
    kk = k * kk_ref[...]
    kk_norm = jnp.sqrt(_dot_precise_r(kk * kk, hsum, 2))
    kk = kk / jnp.maximum(kk_norm, 1e-12)
    k = k * (1.0 + (a_sig - 1.0) * ka_ref[...])
    bonus_o[0] = _dot_precise_r(r * k * rk_ref[...], hsum, 2) * v
    a_vec = -kk
    b_vec = kk * a_sig
    v_o[0] = v.astype(BF16)

    n_c = ts // CHUNK
    for d, (tri_ref, op_o, pc_o) in enumerate(((tril_ref, opf_o, pcf_o), (triu_ref, opb_o, pcb_o))):
        lwd = lw[:, d * dm:(d + 1) * dm]
        cum = _dot_precise_l(tri_ref[...], lwd, 3)
        grow = jnp.exp(cum)
        shrink = jnp.exp(-cum)
        op_o[0, 0] = (r * grow).astype(BF16)
        op_o[0, 1] = (a_vec * jnp.exp(cum - lwd)).astype(BF16)
        op_o[0, 2] = (b_vec * shrink).astype(BF16)
        op_o[0, 3] = (k * shrink).astype(BF16)
        for c in range(n_c):
            row = c * CHUNK + (CHUNK - 1 if d == 0 else 0)
            pc_o[0, c] = grow[row:row + 1, :]


def _rwkv_prep(hrw, w):
    b, s, _ = hrw.shape
    ts = min(256, s)
    n_c = ts // CHUNK
    dm = RWKV_DIM
    rr = jnp.arange(ts)
    same = (rr[:, None] // CHUNK) == (rr[None, :] // CHUNK)
    tril = (same & (rr[None, :] <= rr[:, None])).astype(BF16)
    triu = (same & (rr[None, :] >= rr[:, None])).astype(BF16)
    wl = [w["mu_prev"], w["mu_next"], w["w0"], w["wup"], w["a0"], w["aup"], w["gup"], w["k_k"], w["k_a"],
          w["r_k"], w["hsum"], tril, triu]
    blk8 = ts // 8
    n8 = s // 8
    tok = lambda bi, i: (bi, i, 0)
    return pl.pallas_call(
        _rwkv_prep_kernel,
        out_shape=(
            jax.ShapeDtypeStruct((b, 4, s, dm), BF16),
            jax.ShapeDtypeStruct((b, 4, s, dm), BF16),
            jax.ShapeDtypeStruct((b, s, dm), BF16),
            jax.ShapeDtypeStruct((b, s // CHUNK, 1, dm), F32),
            jax.ShapeDtypeStruct((b, s // CHUNK, 1, dm), F32),
            jax.ShapeDtypeStruct((b, s, dm), F32),
            jax.ShapeDtypeStruct((b, s, dm), F32),
        ),
        grid=(b, s // ts),
        in_specs=[
            pl.BlockSpec((1, ts, RW_PAD), tok),
            pl.BlockSpec((1, 8, RW_PAD), lambda bi, i: (bi, jnp.maximum(i * blk8 - 1, 0), 0)),
            pl.BlockSpec((1, 8, RW_PAD), lambda bi, i: (bi, jnp.minimum((i + 1) * blk8, n8 - 1), 0)),
        ] + [_const_spec(a.shape) for a in wl],
        out_specs=(
            pl.BlockSpec((1, 4, ts, dm), lambda bi, i: (bi, 0, i, 0)),
            pl.BlockSpec((1, 4, ts, dm), lambda bi, i: (bi, 0, i, 0)),
            pl.BlockSpec((1, ts, dm), tok),
            pl.BlockSpec((1, n_c, 1, dm), lambda bi, i: (bi, i, 0, 0)),
            pl.BlockSpec((1, n_c, 1, dm), lambda bi, i: (bi, i, 0, 0)),
            pl.BlockSpec((1, ts, dm), tok),
            pl.BlockSpec((1, ts, dm), tok),
        ),
        compiler_params=_params(("parallel", "parallel")),
        name="rwkv_prep",
    )(hrw, hrw, hrw, *wl)


def _block_diag(x, blockmask):
    return jnp.where(blockmask, jnp.concatenate([x, x], 0), jnp.zeros((), x.dtype))


def _scan_chunk_terms(items, eye2, blockmask):
    bd = lambda x: _block_diag(x, blockmask)
    n = len(items)
    bdv = [bd(it["v"]) for it in items]
    a4 = [_dot_nt(jnp.concatenate([it["at"], it["rt"]], 0), jnp.concatenate([bd(it["bt"]), bd(it["kt"])], 0))
          for it in items]
    a_ab = [jnp.where(it["strict"], a[:CHUNK, :PAIR], 0.0) for it, a in zip(items, a4)]
    a_ak = [jnp.where(it["strict"], a[:CHUNK, PAIR:], 0.0) for it, a in zip(items, a4)]
    a_r = [jnp.where(jnp.concatenate([it["incl"], it["incl"]], 1), a[CHUNK:], 0.0).astype(BF16)
           for it, a in zip(items, a4)]
    w1 = [_dot(a.astype(BF16), b) for a, b in zip(a_ak, bdv)]
    x = [eye2 + a for a in a_ab]
    akb = [a.astype(BF16) for a in a_ab]
    ak = [_dot(a, bd(a)) for a in akb]
    for lvl in range(1, 6):
        akb = [a.astype(BF16) for a in ak]
        if lvl < 5:
            res = [_dot(a, jnp.concatenate([bd(xi.astype(BF16)), bd(a)], 1)) for a, xi in zip(akb, x)]
            x = [xi + r[:, :PAIR] for xi, r in zip(x, res)]
            ak = [r[:, PAIR:] for r in res]
        else:
            x = [xi + _dot(a, bd(xi.astype(BF16))) for a, xi in zip(akb, x)]
    res = [_dot(xi.astype(BF16), jnp.concatenate([bd(it["at"]), bd(w.astype(BF16))], 1))
           for xi, it, w in zip(x, items, w1)]
    ap = [r[:, :PAIR].astype(BF16) for r in res]
    vp = [r[:, PAIR:].astype(BF16) for r in res]
    zero = jnp.zeros((PAIR, PAIR), BF16)
    res = [_dot(a, jnp.concatenate([jnp.concatenate([bd(p_), bd(v_)], 1), jnp.concatenate([zero, b], 1)], 0))
           for a, p_, v_, b in zip(a_r, ap, vp, bdv)]
    rp = [it["rt"].astype(F32) + r[:, :PAIR] for it, r in zip(items, res)]
    y0 = [r[:, PAIR:] for r in res]
    zc = jnp.zeros((CHUNK, PAIR), BF16)
    mg = []
    for it, p_, v_ in zip(items, ap, vp):
        bh = (it["bt"].astype(F32) * it["pc"]).astype(BF16)
        kh = (it["kt"].astype(F32) * it["pc"]).astype(BF16)
        lmat = jnp.concatenate([jnp.concatenate([p_, v_], 1), jnp.concatenate([zc, it["v"]], 1)], 0)
        mg.append(_dot_tn(jnp.concatenate([bh, kh], 0), lmat))
    out = []
    for i in range(n):
        mb = jnp.where(blockmask, mg[i][:, :PAIR], 0.0)
        gb = jnp.where(blockmask, mg[i][:, PAIR:], 0.0)
        lhs = jnp.concatenate([rp[i], mb], 0).astype(BF16)
        out.append((lhs, y0[i], gb))
    return out


def _rwkv_scan_kernel(opf_ref, opb_ref, vf_ref, vb_ref, pcf_ref, pcb_ref, yf_o, yb_o, h_ref, *, cpb):
    i = pl.program_id(1)

    @pl.when(i == 0)
    def _():
        h_ref[...] = jnp.zeros_like(h_ref)

    t_id = lax.broadcasted_iota(jnp.int32, (CHUNK, PAIR), 0)
    s_id = lax.broadcasted_iota(jnp.int32, (CHUNK, PAIR), 1) % CHUNK
    eye2 = (t_id == s_id).astype(F32)
    rb = lax.broadcasted_iota(jnp.int32, (PAIR, PAIR), 0) // RWKV_HEAD
    cb = lax.broadcasted_iota(jnp.int32, (PAIR, PAIR), 1) // RWKV_HEAD
    blockmask = rb == cb
    dirs = ((opf_ref, vf_ref, pcf_ref, yf_o), (opb_ref, vb_ref, pcb_ref, yb_o))
    items = []
    for step in range(cpb):
        for d, (op_ref, v_ref, pc_ref, _) in enumerate(dirs):
            c = step if d == 0 else cpb - 1 - step
            rows = slice(c * CHUNK, (c + 1) * CHUNK)
            for g in range(N_PAIR):
                ln = slice(g * PAIR, (g + 1) * PAIR)
                items.append(dict(
                    rt=op_ref[0, 0, rows, ln], at=op_ref[0, 1, rows, ln], bt=op_ref[0, 2, rows, ln],
                    kt=op_ref[0, 3, rows, ln], v=v_ref[0, rows, ln], pc=pc_ref[0, c, :, ln],
                    strict=(s_id < t_id) if d == 0 else (s_id > t_id),
                    incl=(s_id <= t_id) if d == 0 else (s_id >= t_id)))
    terms = _scan_chunk_terms(items, eye2, blockmask)
    states = [[h_ref[d, g] for g in range(N_PAIR)] for d in range(2)]
    per_step = 2 * N_PAIR
    for step in range(cpb):
        chains = [(d, g) for d in range(2) for g in range(N_PAIR)]
        res = [_dot(terms[step * per_step + n][0], states[d][g].astype(BF16)) for n, (d, g) in enumerate(chains)]
        for n, (d, g) in enumerate(chains):
            _, y0, gb = terms[step * per_step + n]
            pc = items[step * per_step + n]["pc"]
            c = step if d == 0 else cpb - 1 - step
            y_o = dirs[d][3]
            y_o[0, c * CHUNK:(c + 1) * CHUNK, g * PAIR:(g + 1) * PAIR] = res[n][:CHUNK] + y0
            pcol = jnp.transpose(jnp.broadcast_to(pc, (PAIR, PAIR)))
            states[d][g] = pcol * states[d][g] + res[n][CHUNK:] + gb
    for d in range(2):
        for g in range(N_PAIR):
            h_ref[d, g] = states[d][g]


def _rwkv_scan(opf, opb, v, pcf, pcb):
    b, _, s, dm = opf.shape
    cpb = min(2, s // CHUNK)
    cb = cpb * CHUNK
    n_b = s // cb
    fw = lambda bi, i: (bi, i, 0)
    bw = lambda bi, i: (bi, n_b - 1 - i, 0)
    return pl.pallas_call(
        functools.partial(_rwkv_scan_kernel, cpb=cpb),
        out_shape=(jax.ShapeDtypeStruct((b, s, dm), F32), jax.ShapeDtypeStruct((b, s, dm), F32)),
        grid=(b, n_b),
        in_specs=[
            pl.BlockSpec((1, 4, cb, dm), lambda bi, i: (bi, 0, i, 0)),
            pl.BlockSpec((1, 4, cb, dm), lambda bi, i: (bi, 0, n_b - 1 - i, 0)),
            pl.BlockSpec((1, cb, dm), fw),
            pl.BlockSpec((1, cb, dm), bw),
            pl.BlockSpec((1, cpb, 1, dm), lambda bi, i: (bi, i, 0, 0)),
            pl.BlockSpec((1, cpb, 1, dm), lambda bi, i: (bi, n_b - 1 - i, 0, 0)),
        ],
        out_specs=(pl.BlockSpec((1, cb, dm), fw), pl.BlockSpec((1, cb, dm), bw)),
        scratch_shapes=[pltpu.VMEM((2, N_PAIR, PAIR, PAIR), F32)],
        compiler_params=_params(("parallel", "arbitrary")),
        name="rwkv_scan",
    )(opf, opb, v, v, pcf, pcb)


def _mem_kv_kernel(m_ref, g_ref, b_ref, w_ref, k_o, v_o):
    m = _layernorm(m_ref[0], g_ref[...], b_ref[...]).astype(BF16)
    kv = _dot(m, w_ref[...])
    half = X_HEADS * X_HEAD
    for h in range(X_HEADS):
        k_o[0, h] = kv[:, h * X_HEAD:(h + 1) * X_HEAD].astype(BF16)
        v_o[0, h] = kv[:, half + h * X_HEAD:half + (h + 1) * X_HEAD].astype(BF16)


def _mem_kv(mem, w):
    b, ml, d = mem.shape
    shp = jax.ShapeDtypeStruct((b, X_HEADS, ml, X_HEAD), BF16)
    spec = pl.BlockSpec((1, X_HEADS, ml, X_HEAD), lambda bi: (bi, 0, 0, 0))
    return pl.pallas_call(
        _mem_kv_kernel,
        out_shape=(shp, shp),
        grid=(b,),
        in_specs=[pl.BlockSpec((1, ml, d), lambda bi: (bi, 0, 0)), _const_spec((1, d)), _const_spec((1, d)),
                  _const_spec(w["w_ckv"].shape)],
        out_specs=(spec, spec),
        compiler_params=_params(("parallel",)),
        name="mem_kv",
    )(mem, w["mem_g"], w["mem_b"], w["w_ckv"])


def _mix_cross_kernel(x_ref, o_ref, yf_ref, yb_ref, bonus_ref, g_ref, gate_ref, mk_ref, mv_ref,
                      lnxg_ref, lnxb_ref, havg_ref, pmla_ref, prw_ref, wo_ref, ln2g_ref, ln2b_ref,
                      wcq_ref, wco_ref, ln3g_ref, ln3b_ref, out_ref):
    x = x_ref[0]
    havg = havg_ref[...]
    y = yf_ref[0] + yb_ref[0]
    mu = _dot_precise_r(y, havg, 2)
    dlt = y - mu
    var = _dot_precise_r(dlt * dlt, havg, 2)
    yn = dlt * lax.rsqrt(var + GN_EPS) * lnxg_ref[...] + lnxb_ref[...]
    rw = ((yn + bonus_ref[0]) * g_ref[0]).astype(BF16)
    br_b = _dot(rw, prw_ref[...])
    br_a = _dot_tn(o_ref[0], pmla_ref[...])
    gates = gate_ref[0].astype(F32)
    mixin = (gates[:, :D_MODEL] * br_a + gates[:, D_MODEL:] * br_b).astype(BF16)
    x2 = _layernorm(ALPHA * x + _dot(mixin, wo_ref[...]), ln2g_ref[...], ln2b_ref[...])

    q = _dot(x2.astype(BF16), wcq_ref[...]).astype(BF16)
    heads = []
    for h in range(X_HEADS):
        s = _dot_nt(q[:, h * X_HEAD:(h + 1) * X_HEAD], mk_ref[0, h])
        p = jnp.exp(s - jnp.max(s, -1, keepdims=True))
        oh = _dot(p.astype(BF16), mv_ref[0, h]) / jnp.sum(p, -1, keepdims=True)
        heads.append(oh.astype(BF16))
    cross = _dot(jnp.concatenate(heads, 1), wco_ref[...])
    out_ref[0] = _layernorm(ALPHA * x2 + cross, ln3g_ref[...], ln3b_ref[...])


def _mix_cross(x1, o, yf, yb, bonus, g, gates, mk, mv, w):
    b, s, d = x1.shape
    ts = min(256, s)
    dm = RWKV_DIM
    ml = mk.shape[2]
    wl = [w["lnx_g"], w["lnx_b"], w["havg"], w["p_mla"], w["p_rwkv"], w["w_o"], w["ln2_g"], w["ln2_b"],
          w["w_cq"], w["w_co"], w["ln3_g"], w["ln3_b"]]
    tok = lambda bi, i: (bi, i, 0)
    mspec = pl.BlockSpec((1, X_HEADS, ml, X_HEAD), lambda bi, i: (bi, 0, 0, 0))
    return pl.pallas_call(
        _mix_cross_kernel,
        out_shape=jax.ShapeDtypeStruct((b, s, d), F32),
        grid=(b, s // ts),
        in_specs=[
            pl.BlockSpec((1, ts, d), tok),
            pl.BlockSpec((1, MLA_HEADS * V_HEAD, ts), lambda bi, i: (bi, 0, i)),
            pl.BlockSpec((1, ts, dm), tok),
            pl.BlockSpec((1, ts, dm), tok),
            pl.BlockSpec((1, ts, dm), tok),
            pl.BlockSpec((1, ts, dm), tok),
            pl.BlockSpec((1, ts, SEG_GATE), tok),
            mspec, mspec,
        ] + [_const_spec(a.shape) for a in wl],
        out_specs=pl.BlockSpec((1, ts, d), tok),
        compiler_params=_params(("parallel", "parallel")),
        name="mix_cross",
    )(x1, o, yf, yb, bonus, g, gates, mk, mv, *wl)


def _rope_tables(s):
    inv = 1.0 / (ROPE_THETA ** (jnp.arange(0, QK_ROPE, 2, dtype=F32) / QK_ROPE))
    ang = jnp.arange(s, dtype=F32)[:, None] * inv[None, :]
    cos, sin = jnp.cos(ang), jnp.sin(ang)
    cos_t = jnp.ones((s, LANE), F32).at[:, QK_NOPE:QK_NOPE + QK_ROPE].set(jnp.concatenate([cos, cos], 1))
    sin_t = jnp.zeros((s, LANE), F32).at[:, QK_NOPE:QK_NOPE + QK_ROPE].set(jnp.concatenate([sin, sin], 1))
    return cos_t, sin_t


def _rot_cols(w):
    half = QK_ROPE // 2
    return jnp.concatenate([-w[..., half:], w[..., :half]], -1)


def _prep_weights(p):
    row = lambda a: a.reshape(1, -1).astype(F32)
    bf = lambda a: a.astype(BF16)
    w = {}
    for name in ("ln1", "ln2", "ln3", "ln4"):
        w[name + "_g"] = row(p[name + "_g"])
        w[name + "_b"] = row(p[name + "_b"])
    for name in ("ffn1", "ffn2"):
        wgu = p[name + "_wgu"]
        w[name + "_wg"] = bf(wgu[:, :D_FF])
        w[name + "_wu"] = bf(wgu[:, D_FF:])
        w[name + "_wd"] = bf(p[name + "_wd"])

    w_in = p["w_in"]
    d = w_in.shape[0]
    w_kr = w_in[:, OFF_KV + KV_LORA:OFF_RWKV]
    pad_l = jnp.zeros((d, QK_NOPE), F32)
    pad_r = jnp.zeros((d, LANE - QK_NOPE - QK_ROPE), F32)
    w["wa"] = bf(jnp.concatenate([w_in[:, :OFF_KV], w_in[:, OFF_KV:OFF_KV + KV_LORA],
                                  pad_l, w_kr, pad_r, pad_l, _rot_cols(w_kr), pad_r], 1))
    w["q_norm_g"] = row(p["q_norm_g"])
    w["kv_norm_g"] = row(p["kv_norm_g"])
    scale = (QK_NOPE + QK_ROPE) ** -0.5 * LOG2E
    wuq = (p["w_uq"] * scale).reshape(Q_LORA, MLA_HEADS, QK_NOPE + QK_ROPE)
    zpad = jnp.zeros((Q_LORA, MLA_HEADS, LANE - QK_NOPE - QK_ROPE), F32)
    wuq_main = jnp.concatenate([wuq, zpad], -1)
    wuq_rot = jnp.concatenate([jnp.zeros((Q_LORA, MLA_HEADS, QK_NOPE), F32), _rot_cols(wuq[..., QK_NOPE:]), zpad], -1)
    w["wuqt"] = bf(jnp.concatenate([wuq_main.reshape(Q_LORA, -1), wuq_rot.reshape(Q_LORA, -1)], 1).T)
    wukv = p["w_ukv"].reshape(KV_LORA, MLA_HEADS, QK_NOPE + V_HEAD)
    w["wuk"] = bf(jnp.concatenate([wukv[..., :QK_NOPE], jnp.zeros((KV_LORA, MLA_HEADS, LANE - QK_NOPE), F32)],
                                  -1).reshape(KV_LORA, -1))
    w["wuvt"] = bf(wukv[..., QK_NOPE:].reshape(KV_LORA, -1).T)

    w_rw = w_in[:, OFF_RWKV:OFF_GATE]
    r_wd = 3 * RWKV_DIM
    r_ad = r_wd + 2 * DECAY_LORA
    r_gd = r_ad + AAA_LORA

    def regroup(a):
        z = jnp.zeros(a.shape[:-1] + (LANE - AAA_LORA,), a.dtype)
        return jnp.concatenate([a[..., :r_ad], a[..., r_ad:r_gd], z, a[..., r_gd:]], -1)

    w["wrw"] = bf(regroup(w_rw))
    w["mu_prev"] = row(regroup(p["mu_prev"]))
    w["mu_next"] = row(regroup(p["mu_next"]))
    w["wgate"] = bf(w_in[:, OFF_GATE:])
    w["b_gate"] = row(p["b_gate"])
    w["w0"] = row(p["w0"])
    zup = jnp.zeros((DECAY_LORA, RWKV_DIM), F32)
    w["wup"] = bf(jnp.concatenate([jnp.concatenate([p["w_up"][0], zup], 1),
                                   jnp.concatenate([zup, p["w_up"][1]], 1)], 0))
    w["a0"] = row(p["a0"])
    w["aup"] = bf(jnp.concatenate([p["a_up"], jnp.zeros((LANE - AAA_LORA, RWKV_DIM), F32)], 0))
    w["gup"] = bf(p["g_up"])
    w["k_k"] = row(p["k_k"])
    w["k_a"] = row(p["k_a"])
    w["r_k"] = row(p["r_k"])
    hid = jnp.arange(RWKV_DIM) // RWKV_HEAD
    same_head = (hid[:, None] == hid[None, :]).astype(F32)
    w["hsum"] = bf(same_head)
    w["havg"] = bf(same_head / RWKV_HEAD)
    w["lnx_g"] = row(p["lnx_g"])
    w["lnx_b"] = row(p["lnx_b"])
    w["p_mla"] = bf(p["p_mla"])
    w["p_rwkv"] = bf(p["p_rwkv"])
    w["w_o"] = bf(p["w_o"])
    w["mem_g"] = row(p["mem_g"])
    w["mem_b"] = row(p["mem_b"])
    w["w_cq"] = bf(p["w_cq"] * (X_HEAD ** -0.5))
    w["w_ckv"] = bf(p["w_ckv"])
    w["w_co"] = bf(p["w_co"])
    return w


def _layer(x, mem, w):
    b, s, d = x.shape
    cos_t, sin_t = _rope_tables(s)
    x1 = _ffn_ln(x.reshape(b * s, d), w["ffn1_wg"], w["ffn1_wu"], w["ffn1_wd"], w["ln1_g"], w["ln1_b"])
    x1 = x1.reshape(b, s, d)
    q, k, v, hrw, gates = _inproj(x1, cos_t, sin_t, w)
    o = _mla_attention(q, k, v)
    opf, opb, vr, pcf, pcb, bonus, g = _rwkv_prep(hrw, w)
    yf, yb = _rwkv_scan(opf, opb, vr, pcf, pcb)
    mk, mv = _mem_kv(mem, w)
    x3 = _mix_cross(x1, o, yf, yb, bonus, g, gates, mk, mv, w)
    out = _ffn_ln(x3.reshape(b * s, d), w["ffn2_wg"], w["ffn2_wu"], w["ffn2_wd"], w["ln4_g"], w["ln4_b"])
    return out.reshape(b, s, d)


_PARAM_NAMES = ("ln1_g", "ln1_b", "ffn1_wgu", "ffn1_wd", "w_in", "b_gate", "q_norm_g", "w_uq", "kv_norm_g",
                "w_ukv", "p_mla", "mu_prev", "mu_next", "w0", "w_up", "a0", "a_up", "g_up", "k_k", "k_a", "r_k",
                "lnx_g", "lnx_b", "p_rwkv", "w_o", "ln2_g", "ln2_b", "mem_g", "mem_b", "w_cq", "w_ckv", "w_co",
                "ln3_g", "ln3_b", "ffn2_wgu", "ffn2_wd", "ln4_g", "ln4_b")


def kernel(x_prompt, x_sample, mem_prompt, mem_sample, ln1_g, ln1_b, ffn1_wgu, ffn1_wd, w_in, b_gate, q_norm_g, w_uq, kv_norm_g, w_ukv, p_mla, mu_prev, mu_next, w0, w_up, a0, a_up, g_up, k_k, k_a, r_k, lnx_g, lnx_b, p_rwkv, w_o, ln2_g, ln2_b, mem_g, mem_b, w_cq, w_ckv, w_co, ln3_g, ln3_b, ffn2_wgu, ffn2_wd, ln4_g, ln4_b):
    stacked = dict(zip(_PARAM_NAMES, (ln1_g, ln1_b, ffn1_wgu, ffn1_wd, w_in, b_gate, q_norm_g, w_uq, kv_norm_g,
                                      w_ukv, p_mla, mu_prev, mu_next, w0, w_up, a0, a_up, g_up, k_k, k_a, r_k,
                                      lnx_g, lnx_b, p_rwkv, w_o, ln2_g, ln2_b, mem_g, mem_b, w_cq, w_ckv, w_co,
                                      ln3_g, ln3_b, ffn2_wgu, ffn2_wd, ln4_g, ln4_b)))
    y_prompt, y_sample = x_prompt, x_sample
    for layer in range(DEPTH):
        w = _prep_weights({name: a[layer] for name, a in stacked.items()})
        y_prompt = _layer(y_prompt, mem_prompt, w)
        y_sample = _layer(y_sample, mem_sample, w)
    return (y_prompt, y_sample)
```
